```python
import jax, jax.numpy as jnp
from jax import lax
import numpy as np

D_MODEL = 2048
BATCH = 4
SEQ = 4096
DEPTH = 1

N_ATT_HEADS = 8
HEAD_DIM = 128
ATT_WIDTH = N_ATT_HEADS * HEAD_DIM
MOBA_BLOCK = 256
MOBA_TOPK = 3
Q_CHUNK = 128
N_SGU_GROUPS = 8
SGU_GROUP_DIM = 128
SGU_WIDTH = N_SGU_GROUPS * SGU_GROUP_DIM
SGU_CHUNK = 128
IN_SPLITS = (ATT_WIDTH, ATT_WIDTH, ATT_WIDTH, ATT_WIDTH,
             SGU_WIDTH, SGU_WIDTH, SGU_WIDTH,
             D_MODEL, D_MODEL)
IN_WIDTH = 4 * ATT_WIDTH + 3 * SGU_WIDTH + 2 * D_MODEL
EPS = 1e-6
NEG_INF = -1e30

kernel_name = "hybrid_moba_gmlp_gated_layer"


def rms_norm(x, g):
    xf = x.astype(jnp.float32)
    y = xf * lax.rsqrt(jnp.mean(xf * xf, axis=-1, keepdims=True) + EPS)
    return (y * g.astype(jnp.float32)).astype(x.dtype)


def moba_attention(q, k, v):
    B, S, H, Dh = q.shape
    nb = -(-S // MOBA_BLOCK)
    s_pad = nb * MOBA_BLOCK
    topk = min(MOBA_TOPK, nb)
    nc = S // Q_CHUNK
    scale = Dh ** -0.5

    qh = q.transpose(0, 2, 1, 3)
    pad = ((0, 0), (0, 0), (0, s_pad - S), (0, 0))
    kh = jnp.pad(k.transpose(0, 2, 1, 3), pad)
    vh = jnp.pad(v.transpose(0, 2, 1, 3), pad)
    k_blocks = kh.reshape(B, H, nb, MOBA_BLOCK, Dh)
    v_blocks = vh.reshape(B, H, nb, MOBA_BLOCK, Dh)
    k_mean = jnp.mean(k_blocks.astype(jnp.float32), axis=3)
    q_chunks = qh.reshape(B, H, nc, Q_CHUNK, Dh).transpose(2, 0, 1, 3, 4)

    bi = jnp.arange(B)[:, None, None, None]
    hi = jnp.arange(H)[None, :, None, None]
    blk_ids = jnp.arange(nb)

    def one_chunk(args):
        qc, c = args
        q_pos = c * Q_CHUNK + jnp.arange(Q_CHUNK)
        cur = (c * Q_CHUNK) // MOBA_BLOCK
        gate = jnp.einsum('bhqd,bhnd->bhqn', qc.astype(jnp.float32), k_mean)
        gate = jnp.where(blk_ids < cur, gate, NEG_INF)
        _, idx = lax.top_k(gate, topk)
        valid = idx < cur
        k_sel = k_blocks[bi, hi, idx]
        v_sel = v_blocks[bi, hi, idx]
        s_past = jnp.einsum('bhqd,bhqknd->bhqkn', qc, k_sel,
                            preferred_element_type=jnp.float32) * scale
        s_past = jnp.where(valid[..., None], s_past, NEG_INF)
        k_own = lax.dynamic_index_in_dim(k_blocks, cur, axis=2, keepdims=False)
        v_own = lax.dynamic_index_in_dim(v_blocks, cur, axis=2, keepdims=False)
        s_own = jnp.einsum('bhqd,bhnd->bhqn', qc, k_own,
                           preferred_element_type=jnp.float32) * scale
        key_pos = cur * MOBA_BLOCK + jnp.arange(MOBA_BLOCK)
        s_own = jnp.where(key_pos[None, :] <= q_pos[:, None], s_own, NEG_INF)
        logits = jnp.concatenate(
            [s_past.reshape(B, H, Q_CHUNK, topk * MOBA_BLOCK), s_own], axis=-1)
        p = jax.nn.softmax(logits, axis=-1)
        p_past = p[..., :topk * MOBA_BLOCK].reshape(B, H, Q_CHUNK, topk, MOBA_BLOCK)
        p_own = p[..., topk * MOBA_BLOCK:]
        o = (jnp.einsum('bhqkn,bhqknd->bhqd', p_past.astype(v_sel.dtype), v_sel)
             + jnp.einsum('bhqn,bhnd->bhqd', p_own.astype(v_own.dtype), v_own))
        return o.astype(qc.dtype)

    out = lax.map(one_chunk, (q_chunks, jnp.arange(nc, dtype=jnp.int32)))
    return out.transpose(1, 0, 3, 2, 4).reshape(B, S, H * Dh)


def spatial_gating(u, v, v_norm_g, w_s, b_s):
    B, S, _ = u.shape
    n = S // SGU_CHUNK
    vg = rms_norm(v.reshape(B, S, N_SGU_GROUPS, SGU_GROUP_DIM), v_norm_g)
    vg = vg.reshape(B, n, SGU_CHUNK, N_SGU_GROUPS, SGU_GROUP_DIM)
    causal = jnp.tril(jnp.ones((SGU_CHUNK, SGU_CHUNK), dtype=bool))
    w = jnp.where(causal[None], w_s, jnp.zeros((), w_s.dtype))
    mixed = jnp.einsum('gts,bnsgc->bntgc', w, vg) + b_s.T[:, :, None]
    return u * mixed.reshape(B, S, SGU_WIDTH)


def hybrid_layer(x, norm_g, w_in, q_norm_g, k_norm_g, sgu_norm_g, w_s, b_s,
                 w_proj_a, w_proj_b, w_out):
    B, S, _ = x.shape
    h = rms_norm(x, norm_g)
    proj = h @ w_in
    cuts = [int(c) for c in np.cumsum(IN_SPLITS)[:-1]]
    q, k, v, z_a, u_b, v_b, z_b, g_a, g_b = jnp.split(proj, cuts, axis=-1)
    q = rms_norm(q.reshape(B, S, N_ATT_HEADS, HEAD_DIM), q_norm_g)
    k = rms_norm(k.reshape(B, S, N_ATT_HEADS, HEAD_DIM), k_norm_g)
    v = v.reshape(B, S, N_ATT_HEADS, HEAD_DIM)
    att = moba_attention(q, k, v)
    y_a = (att * jax.nn.silu(z_a)) @ w_proj_a
    sgu = spatial_gating(jax.nn.gelu(u_b), jax.nn.gelu(v_b), sgu_norm_g, w_s, b_s)
    y_b = (sgu * jax.nn.silu(z_b)) @ w_proj_b
    merged = jax.nn.sigmoid(g_a) * y_a + jax.nn.sigmoid(g_b) * y_b
    return x + merged @ w_out


def setup_inputs(seed: int = 0) -> dict:
    key = jax.random.key(seed)
    ks = jax.random.split(key, 12)
    f32 = jnp.float32
    x = jax.random.normal(ks[0], (BATCH, SEQ, D_MODEL), f32)
    norm_g = 1.0 + 0.02 * jax.random.normal(ks[1], (DEPTH, D_MODEL), f32)
    w_in = jax.random.normal(ks[2], (DEPTH, D_MODEL, IN_WIDTH), f32) * D_MODEL ** -0.5
    q_norm_g = 1.0 + 0.02 * jax.random.normal(ks[3], (DEPTH, HEAD_DIM), f32)
    k_norm_g = 1.0 + 0.02 * jax.random.normal(ks[4], (DEPTH, HEAD_DIM), f32)
    sgu_norm_g = 1.0 + 0.02 * jax.random.normal(ks[5], (DEPTH, SGU_GROUP_DIM), f32)
    w_spatial = jax.random.normal(ks[6], (DEPTH, N_SGU_GROUPS, SGU_CHUNK, SGU_CHUNK), f32) * SGU_CHUNK ** -0.5
    b_spatial = 1.0 + 0.1 * jax.random.normal(ks[7], (DEPTH, N_SGU_GROUPS, SGU_CHUNK), f32)
    w_proj_a = jax.random.normal(ks[8], (DEPTH, ATT_WIDTH, D_MODEL), f32) * ATT_WIDTH ** -0.5
    w_proj_b = jax.random.normal(ks[9], (DEPTH, SGU_WIDTH, D_MODEL), f32) * SGU_WIDTH ** -0.5
    w_out = jax.random.normal(ks[10], (DEPTH, D_MODEL, D_MODEL), f32) * D_MODEL ** -0.5
    return {"x": x, "norm_g": norm_g, "w_in": w_in, "q_norm_g": q_norm_g,
            "k_norm_g": k_norm_g, "sgu_norm_g": sgu_norm_g, "w_spatial": w_spatial,
            "b_spatial": b_spatial, "w_proj_a": w_proj_a, "w_proj_b": w_proj_b,
            "w_out": w_out}


def reference(x, norm_g, w_in, q_norm_g, k_norm_g, sgu_norm_g, w_spatial, b_spatial,
              w_proj_a, w_proj_b, w_out):
    for l in range(DEPTH):
        x = hybrid_layer(x, norm_g[l], w_in[l], q_norm_g[l], k_norm_g[l], sgu_norm_g[l],
                         w_spatial[l], b_spatial[l], w_proj_a[l], w_proj_b[l], w_out[l])
    return x
```

```python
import functools

import jax
import jax.numpy as jnp
from jax import lax
from jax.experimental import pallas as pl
from jax.experimental.pallas import tpu as pltpu

D_MODEL = 2048
N_ATT_HEADS = 8
HEAD_DIM = 128
ATT_WIDTH = N_ATT_HEADS * HEAD_DIM
MOBA_BLOCK = 256
MOBA_TOPK = 3
N_SGU_GROUPS = 8
SGU_GROUP_DIM = 128
SGU_WIDTH = N_SGU_GROUPS * SGU_GROUP_DIM
SGU_CHUNK = 128
IN_WIDTH = 4 * ATT_WIDTH + 3 * SGU_WIDTH + 2 * D_MODEL
EPS = 1e-6
NEG_INF = -1e30

SEG = 1024
N_SEG = IN_WIDTH // SEG
ACT_WIDTH = IN_WIDTH - 3 * SEG

NORM_ROWS = 512
PROJ_ROWS = 1024
PROJ_CHUNK = 256
TAIL_ROWS = 256

VMEM_LIMIT = 56 * 1024 * 1024

_NT = (((1,), (1,)), ((), ()))
_TN = (((0,), (0,)), ((), ()))


def _prenorm_kernel(x_ref, g_ref, h_ref):
    x = x_ref[...]
    ms = jnp.mean(x * x, axis=-1, keepdims=True)
    h_ref[...] = (x * lax.rsqrt(ms + EPS) * g_ref[...]).astype(h_ref.dtype)


def _prenorm(x2, g):
    t = x2.shape[0]
    return pl.pallas_call(
        _prenorm_kernel,
        grid=(t // NORM_ROWS,),
        in_specs=[
            pl.BlockSpec((NORM_ROWS, D_MODEL), lambda i: (i, 0)),
            pl.BlockSpec((1, D_MODEL), lambda i: (0, 0)),
        ],
        out_specs=pl.BlockSpec((NORM_ROWS, D_MODEL), lambda i: (i, 0)),
        out_shape=jax.ShapeDtypeStruct((t, D_MODEL), jnp.bfloat16),
        compiler_params=pltpu.CompilerParams(
            dimension_semantics=("arbitrary",), vmem_limit_bytes=VMEM_LIMIT),
        name="prenorm",
    )(x2, g.reshape(1, D_MODEL))


def _group_rms(y, g_row):
    outs = []
    for c in range(SEG // HEAD_DIM):
        yc = y[:, c * HEAD_DIM:(c + 1) * HEAD_DIM]
        ms = jnp.mean(yc * yc, axis=-1, keepdims=True)
        outs.append(yc * lax.rsqrt(ms + EPS) * g_row)
    return jnp.concatenate(outs, axis=-1)


def _proj_kernel(h_ref, w_ref, qg_ref, kg_ref, sg_ref,
                 q_ref, k_ref, kmean_ref, v_ref, act_ref):
    j = pl.program_id(1)
    n_chunks = PROJ_ROWS // PROJ_CHUNK

    def chunk_dot(r):
        rows = pl.ds(pl.multiple_of(r * PROJ_CHUNK, PROJ_CHUNK), PROJ_CHUNK)
        return rows, jnp.dot(h_ref[rows, :], w_ref[...],
                             preferred_element_type=jnp.float32)

    def for_chunks(fn):
        def body(r, carry):
            rows, y = chunk_dot(r)
            fn(r, rows, y)
            return carry
        lax.fori_loop(0, n_chunks, body, 0)

    @pl.when(j == 0)
    def _():
        def fn(r, rows, y):
            q_ref[rows, :] = _group_rms(y, qg_ref[...])
        for_chunks(fn)

    @pl.when(j == 1)
    def _():
        def fn(r, rows, y):
            kn = _group_rms(y, kg_ref[...])
            k_ref[rows, :] = kn.astype(k_ref.dtype)
            kmean_ref[0, pl.ds(r, 1), :] = jnp.mean(kn, axis=0, keepdims=True)
        for_chunks(fn)

    @pl.when(j == 2)
    def _():
        def fn(r, rows, y):
            v_ref[rows, :] = y.astype(v_ref.dtype)
        for_chunks(fn)

    @pl.when((j == 3) | (j == 6))
    def _():
        def fn(r, rows, y):
            act_ref[rows, :] = jax.nn.silu(y).astype(act_ref.dtype)
        for_chunks(fn)

    @pl.when(j == 4)
    def _():
        def fn(r, rows, y):
            act_ref[rows, :] = jax.nn.gelu(y, approximate=True).astype(act_ref.dtype)
        for_chunks(fn)

    @pl.when(j == 5)
    def _():
        def fn(r, rows, y):
            vg = _group_rms(jax.nn.gelu(y, approximate=True), sg_ref[...])
            act_ref[rows, :] = vg.astype(act_ref.dtype)
        for_chunks(fn)

    @pl.when(j >= 7)
    def _():
        def fn(r, rows, y):
            act_ref[rows, :] = jax.nn.sigmoid(y).astype(act_ref.dtype)
        for_chunks(fn)


def _proj(h, w_bf, qg, kg, sg):
    t = h.shape[0]
    n_row_tiles = t // PROJ_ROWS
    blocks_per_tile = PROJ_ROWS // MOBA_BLOCK
    row_spec = lambda width: pl.BlockSpec((PROJ_ROWS, width), lambda i, j: (i, 0))
    gain_spec = pl.BlockSpec((1, HEAD_DIM), lambda i, j: (0, 0))
    return pl.pallas_call(
        _proj_kernel,
        grid=(n_row_tiles, N_SEG),
        in_specs=[
            row_spec(D_MODEL),
            pl.BlockSpec((D_MODEL, SEG), lambda i, j: (0, j)),
            gain_spec, gain_spec, gain_spec,
        ],
        out_specs=[
            row_spec(SEG),
            row_spec(SEG),
            pl.BlockSpec((1, blocks_per_tile, SEG), lambda i, j: (i, 0, 0)),
            row_spec(SEG),
            pl.BlockSpec((PROJ_ROWS, SEG), lambda i, j: (i, jnp.maximum(j - 3, 0))),
        ],
        out_shape=[
            jax.ShapeDtypeStruct((t, SEG), jnp.float32),
            jax.ShapeDtypeStruct((t, SEG), jnp.bfloat16),
            jax.ShapeDtypeStruct((n_row_tiles, blocks_per_tile, SEG), jnp.float32),
            jax.ShapeDtypeStruct((t, SEG), jnp.bfloat16),
            jax.ShapeDtypeStruct((t, ACT_WIDTH), jnp.bfloat16),
        ],
        compiler_params=pltpu.CompilerParams(
            dimension_semantics=("arbitrary", "arbitrary"),
            vmem_limit_bytes=VMEM_LIMIT),
        name="in_proj",
    )(h, w_bf, qg.reshape(1, HEAD_DIM), kg.reshape(1, HEAD_DIM), sg.reshape(1, SGU_GROUP_DIM))


def _attn_kernel(q_ref, k_ref, v_ref, kmean_ref, o_ref,
                 qs_ref, bias_ref, m_ref, l_ref, acc_ref):
    qi = pl.program_id(2)
    nb = kmean_ref.shape[1]
    q = q_ref[...]

    gate = lax.dot_general(kmean_ref[0], q, _NT, precision=lax.Precision.HIGHEST,
                           preferred_element_type=jnp.float32)
    blk = lax.broadcasted_iota(jnp.int32, gate.shape, 0)
    past = blk < qi
    gate = jnp.where(past, gate, NEG_INF)
    picked = jnp.zeros(gate.shape, jnp.float32)
    for _ in range(min(MOBA_TOPK, nb)):
        best = jnp.max(gate, axis=0, keepdims=True)
        first = jnp.min(jnp.where(gate == best, blk, nb), axis=0, keepdims=True)
        hit = blk == first
        picked = jnp.where(hit, 1.0, picked)
        gate = jnp.where(hit, -jnp.inf, gate)
    bias_ref[...] = jnp.where(past, jnp.where(picked > 0.0, 0.0, NEG_INF), NEG_INF)

    qs = (q * (HEAD_DIM ** -0.5)).astype(jnp.bfloat16)
    qs_ref[...] = qs

    own = pl.ds(pl.multiple_of(qi * MOBA_BLOCK, MOBA_BLOCK), MOBA_BLOCK)
    s = lax.dot_general(k_ref[own, :], qs, _NT, preferred_element_type=jnp.float32)
    key_pos = lax.broadcasted_iota(jnp.int32, s.shape, 0)
    q_pos = lax.broadcasted_iota(jnp.int32, s.shape, 1)
    s = jnp.where(key_pos <= q_pos, s, NEG_INF)
    m0 = jnp.max(s, axis=0, keepdims=True)
    p = jnp.exp(s - m0)
    m_ref[...] = m0
    l_ref[...] = jnp.sum(p, axis=0, keepdims=True)
    acc_ref[...] = lax.dot_general(v_ref[own, :], p.astype(jnp.bfloat16), _TN,
                                   preferred_element_type=jnp.float32)

    def past_block(j, carry):
        rows = pl.ds(pl.multiple_of(j * MOBA_BLOCK, MOBA_BLOCK), MOBA_BLOCK)
        sj = lax.dot_general(k_ref[rows, :], qs_ref[...], _NT,
                             preferred_element_type=jnp.float32)
        sj = sj + bias_ref[pl.ds(j, 1), :]
        m_old = m_ref[...]
        m_new = jnp.maximum(m_old, jnp.max(sj, axis=0, keepdims=True))
        alpha = jnp.exp(m_old - m_new)
        pj = jnp.exp(sj - m_new)
        m_ref[...] = m_new
        l_ref[...] = alpha * l_ref[...] + jnp.sum(pj, axis=0, keepdims=True)
        acc_ref[...] = alpha * acc_ref[...] + lax.dot_general(
            v_ref[rows, :], pj.astype(jnp.bfloat16), _TN,
            preferred_element_type=jnp.float32)
        return carry

    lax.fori_loop(0, qi, past_block, 0)

    o_ref[...] = (acc_ref[...] / l_ref[...]).T.astype(o_ref.dtype)


def _attention(q, k, v, kmean, batch, seq):
    nb = seq // MOBA_BLOCK
    return pl.pallas_call(
        _attn_kernel,
        grid=(batch, N_ATT_HEADS, nb),
        in_specs=[
            pl.BlockSpec((MOBA_BLOCK, HEAD_DIM), lambda b, h, i: (b * nb + i, h)),
            pl.BlockSpec((seq, HEAD_DIM), lambda b, h, i: (b, h)),
            pl.BlockSpec((seq, HEAD_DIM), lambda b, h, i: (b, h)),
            pl.BlockSpec((1, nb, HEAD_DIM), lambda b, h, i: (b, 0, h)),
        ],
        out_specs=pl.BlockSpec((MOBA_BLOCK, HEAD_DIM), lambda b, h, i: (b * nb + i, h)),
        out_shape=jax.ShapeDtypeStruct((batch * seq, ATT_WIDTH), jnp.float32),
        scratch_shapes=[
            pltpu.VMEM((MOBA_BLOCK, HEAD_DIM), jnp.bfloat16),
            pltpu.VMEM((nb, MOBA_BLOCK), jnp.float32),
            pltpu.VMEM((1, MOBA_BLOCK), jnp.float32),
            pltpu.VMEM((1, MOBA_BLOCK), jnp.float32),
            pltpu.VMEM((HEAD_DIM, MOBA_BLOCK), jnp.float32),
        ],
        compiler_params=pltpu.CompilerParams(
            dimension_semantics=("arbitrary", "arbitrary", "arbitrary"),
            vmem_limit_bytes=VMEM_LIMIT),
        name="moba_attention",
    )(q, k, v, kmean)


def _tail_kernel(att_ref, sza_ref, gu_ref, vg_ref, szb_ref, sga_ref, sgb_ref, x_ref,
                 ws_ref, bst_ref, wpa_ref, wpb_ref, wout_ref, o_ref, b_in_ref):
    n_chunks = TAIL_ROWS // SGU_CHUNK
    a_in = (att_ref[...] * sza_ref[...].astype(jnp.float32)).astype(jnp.bfloat16)
    y_a = jnp.dot(a_in, wpa_ref[...], preferred_element_type=jnp.float32)

    t_pos = lax.broadcasted_iota(jnp.int32, (SGU_CHUNK, SGU_CHUNK), 0)
    s_pos = lax.broadcasted_iota(jnp.int32, (SGU_CHUNK, SGU_CHUNK), 1)
    causal = s_pos <= t_pos
    for g in range(N_SGU_GROUPS):
        cols = slice(g * SGU_GROUP_DIM, (g + 1) * SGU_GROUP_DIM)
        w = jnp.where(causal, ws_ref[g], 0.0).astype(jnp.bfloat16)
        vcat = jnp.concatenate(
            [vg_ref[c * SGU_CHUNK:(c + 1) * SGU_CHUNK, cols] for c in range(n_chunks)], axis=1)
        mixed = jnp.dot(w, vcat, preferred_element_type=jnp.float32) + bst_ref[:, g:g + 1]
        for c in range(n_chunks):
            rows = slice(c * SGU_CHUNK, (c + 1) * SGU_CHUNK)
            sgu = gu_ref[rows, cols].astype(jnp.float32) * mixed[:, c * SGU_CHUNK:(c + 1) * SGU_CHUNK]
            b_in_ref[rows, cols] = (sgu * szb_ref[rows, cols].astype(jnp.float32)).astype(jnp.bfloat16)
    y_b = jnp.dot(b_in_ref[...], wpb_ref[...], preferred_element_type=jnp.float32)

    merged = (sga_ref[...].astype(jnp.float32) * y_a
              + sgb_ref[...].astype(jnp.float32) * y_b).astype(jnp.bfloat16)
    o_ref[...] = x_ref[...] + jnp.dot(merged, wout_ref[...], preferred_element_type=jnp.float32)


def _tail(att, act, x2, w_s, b_s_t, wpa, wpb, wout):
    t = x2.shape[0]
    act_spec = lambda width, col: pl.BlockSpec((TAIL_ROWS, width), lambda i: (i, col))
    const = lambda shape: pl.BlockSpec(shape, lambda i: (0,) * len(shape),
                                       pipeline_mode=pl.Buffered(1))
    return pl.pallas_call(
        _tail_kernel,
        grid=(t // TAIL_ROWS,),
        in_specs=[
            act_spec(ATT_WIDTH, 0),
            act_spec(SEG, 0),
            act_spec(SEG, 1),
            act_spec(SEG, 2),
            act_spec(SEG, 3),
            act_spec(D_MODEL, 2),
            act_spec(D_MODEL, 3),
            act_spec(D_MODEL, 0),
            const((N_SGU_GROUPS, SGU_CHUNK, SGU_CHUNK)),
            const((SGU_CHUNK, N_SGU_GROUPS)),
            const((ATT_WIDTH, D_MODEL)),
            const((SGU_WIDTH, D_MODEL)),
            const((D_MODEL, D_MODEL)),
        ],
        out_specs=pl.BlockSpec((TAIL_ROWS, D_MODEL), lambda i: (i, 0)),
        out_shape=jax.ShapeDtypeStruct((t, D_MODEL), jnp.float32),
        scratch_shapes=[pltpu.VMEM((TAIL_ROWS, SGU_WIDTH), jnp.bfloat16)],
        compiler_params=pltpu.CompilerParams(
            dimension_semantics=("arbitrary",), vmem_limit_bytes=VMEM_LIMIT),
        name="tail",
    )(att, act, act, act, act, act, act, x2, w_s, b_s_t, wpa, wpb, wout)


def _layer(x2, batch, seq, norm_g, w_in, q_norm_g, k_norm_g, sgu_norm_g, w_s, b_s,
           w_proj_a, w_proj_b, w_out):
    bf16 = jnp.bfloat16
    h = _prenorm(x2, norm_g)
    q, k, kmean, v, act = _proj(h, w_in.astype(bf16), q_norm_g, k_norm_g, sgu_norm_g)
    kmean = kmean.reshape(batch, seq // MOBA_BLOCK, ATT_WIDTH)
    att = _attention(q, k, v, kmean, batch, seq)
    return _tail(att, act, x2, w_s, b_s.T, w_proj_a.astype(bf16), w_proj_b.astype(bf16),
                 w_out.astype(bf16))


def kernel(x, norm_g, w_in, q_norm_g, k_norm_g, sgu_norm_g, w_spatial, b_spatial,
           w_proj_a, w_proj_b, w_out):
    batch, seq, d = x.shape
    x2 = x.reshape(batch * seq, d)
    for l in range(norm_g.shape[0]):
        x2 = _layer(x2, batch, seq, norm_g[l], w_in[l], q_norm_g[l], k_norm_g[l],
                    sgu_norm_g[l], w_spatial[l], b_spatial[l], w_proj_a[l], w_proj_b[l],
                    w_out[l])
    return x2.reshape(batch, seq, d)
```

```python
import functools

import jax
import jax.numpy as jnp
from jax import lax
from jax.experimental import pallas as pl
from jax.experimental.pallas import tpu as pltpu

D_MODEL = 2048
N_ATT_HEADS = 8
HEAD_DIM = 128
ATT_WIDTH = N_ATT_HEADS * HEAD_DIM
MOBA_BLOCK = 256
MOBA_TOPK = 3
N_SGU_GROUPS = 8
SGU_GROUP_DIM = 128
SGU_WIDTH = N_SGU_GROUPS * SGU_GROUP_DIM
SGU_CHUNK = 128
IN_WIDTH = 4 * ATT_WIDTH + 3 * SGU_WIDTH + 2 * D_MODEL
EPS = 1e-6
NEG_INF = -1e30
LOG2_E = 1.4426950408889634

SEG = 1024
N_SEG = IN_WIDTH // SEG
ACT_WIDTH = IN_WIDTH - 3 * SEG

NORM_ROWS = 512
PROJ_ROWS = 1024
PROJ_CHUNK = 256
TAIL_ROWS = 256
ATTN_HEADS_PER_STEP = 8

VMEM_LIMIT = 56 * 1024 * 1024

_NT = (((1,), (1,)), ((), ()))
_TN = (((0,), (0,)), ((), ()))


def _prenorm_kernel(x_ref, g_ref, h_ref):
    x = x_ref[...]
    ms = jnp.mean(x * x, axis=-1, keepdims=True)
    h_ref[...] = (x * lax.rsqrt(ms + EPS) * g_ref[...]).astype(h_ref.dtype)


def _prenorm(x2, g):
    t = x2.shape[0]
    return pl.pallas_call(
        _prenorm_kernel,
        grid=(t // NORM_ROWS,),
        in_specs=[
            pl.BlockSpec((NORM_ROWS, D_MODEL), lambda i: (i, 0)),
            pl.BlockSpec((1, D_MODEL), lambda i: (0, 0)),
        ],
        out_specs=pl.BlockSpec((NORM_ROWS, D_MODEL), lambda i: (i, 0)),
        out_shape=jax.ShapeDtypeStruct((t, D_MODEL), jnp.bfloat16),
        compiler_params=pltpu.CompilerParams(
            dimension_semantics=("arbitrary",), vmem_limit_bytes=VMEM_LIMIT),
        name="prenorm",
    )(x2, g.reshape(1, D_MODEL))


def _group_rms(y, g_row):
    outs = []
    for c in range(SEG // HEAD_DIM):
        yc = y[:, c * HEAD_DIM:(c + 1) * HEAD_DIM]
        ms = jnp.mean(yc * yc, axis=-1, keepdims=True)
        outs.append(yc * lax.rsqrt(ms + EPS) * g_row)
    return jnp.concatenate(outs, axis=-1)


def _proj_kernel(h_ref, w_ref, qg_ref, kg_ref, sg_ref,
                 q_ref, k_ref, kmean_ref, v_ref, act_ref):
    j = pl.program_id(1)
    n_chunks = PROJ_ROWS // PROJ_CHUNK

    def chunk_dot(r):
        rows = pl.ds(pl.multiple_of(r * PROJ_CHUNK, PROJ_CHUNK), PROJ_CHUNK)
        return rows, jnp.dot(h_ref[rows, :], w_ref[...],
                             preferred_element_type=jnp.float32)

    def for_chunks(fn):
        def body(r, carry):
            rows, y = chunk_dot(r)
            fn(r, rows, y)
            return carry
        lax.fori_loop(0, n_chunks, body, 0)

    @pl.when(j == 0)
    def _():
        def fn(r, rows, y):
            q_ref[rows, :] = _group_rms(y, qg_ref[...])
        for_chunks(fn)

    @pl.when(j == 1)
    def _():
        def fn(r, rows, y):
            kn = _group_rms(y, kg_ref[...])
            k_ref[rows, :] = kn.astype(k_ref.dtype)
            kmean_ref[0, pl.ds(r, 1), :] = jnp.mean(kn, axis=0, keepdims=True)
        for_chunks(fn)

    @pl.when(j == 2)
    def _():
        def fn(r, rows, y):
            v_ref[rows, :] = y.astype(v_ref.dtype)
        for_chunks(fn)

    @pl.when((j == 3) | (j == 6))
    def _():
        def fn(r, rows, y):
            act_ref[rows, :] = jax.nn.silu(y).astype(act_ref.dtype)
        for_chunks(fn)

    @pl.when(j == 4)
    def _():
        def fn(r, rows, y):
            act_ref[rows, :] = jax.nn.gelu(y, approximate=True).astype(act_ref.dtype)
        for_chunks(fn)

    @pl.when(j == 5)
    def _():
        def fn(r, rows, y):
            vg = _group_rms(jax.nn.gelu(y, approximate=True), sg_ref[...])
            act_ref[rows, :] = vg.astype(act_ref.dtype)
        for_chunks(fn)

    @pl.when(j >= 7)
    def _():
        def fn(r, rows, y):
            act_ref[rows, :] = jax.nn.sigmoid(y).astype(act_ref.dtype)
        for_chunks(fn)


def _proj(h, w_bf, qg, kg, sg):
    t = h.shape[0]
    n_row_tiles = t // PROJ_ROWS
    blocks_per_tile = PROJ_ROWS // MOBA_BLOCK
    row_spec = lambda width: pl.BlockSpec((PROJ_ROWS, width), lambda i, j: (i, 0))
    gain_spec = pl.BlockSpec((1, HEAD_DIM), lambda i, j: (0, 0))
    return pl.pallas_call(
        _proj_kernel,
        grid=(n_row_tiles, N_SEG),
        in_specs=[
            row_spec(D_MODEL),
            pl.BlockSpec((D_MODEL, SEG), lambda i, j: (0, j)),
            gain_spec, gain_spec, gain_spec,
        ],
        out_specs=[
            row_spec(SEG),
            row_spec(SEG),
            pl.BlockSpec((1, blocks_per_tile, SEG), lambda i, j: (i, 0, 0)),
            row_spec(SEG),
            pl.BlockSpec((PROJ_ROWS, SEG), lambda i, j: (i, jnp.maximum(j - 3, 0))),
        ],
        out_shape=[
            jax.ShapeDtypeStruct((t, SEG), jnp.float32),
            jax.ShapeDtypeStruct((t, SEG), jnp.bfloat16),
            jax.ShapeDtypeStruct((n_row_tiles, blocks_per_tile, SEG), jnp.float32),
            jax.ShapeDtypeStruct((t, SEG), jnp.bfloat16),
            jax.ShapeDtypeStruct((t, ACT_WIDTH), jnp.bfloat16),
        ],
        compiler_params=pltpu.CompilerParams(
            dimension_semantics=("arbitrary", "arbitrary"),
            vmem_limit_bytes=VMEM_LIMIT),
        name="in_proj",
    )(h, w_bf, qg.reshape(1, HEAD_DIM), kg.reshape(1, HEAD_DIM), sg.reshape(1, SGU_GROUP_DIM))


def _attn_kernel(q_ref, k_ref, v_ref, kmean_ref, o_ref,
                 qs_ref, bias_ref, s_ref, mblk_ref, m_ref, l_ref, acc_ref):
    qi = pl.program_id(2)
    nb = kmean_ref.shape[1]
    n_heads = q_ref.shape[1] // HEAD_DIM
    head_cols = [slice(h * HEAD_DIM, (h + 1) * HEAD_DIM) for h in range(n_heads)]
    own = pl.ds(pl.multiple_of(qi * MOBA_BLOCK, MOBA_BLOCK), MOBA_BLOCK)
    blk = lax.broadcasted_iota(jnp.int32, (nb, MOBA_BLOCK), 0)
    past = blk < qi

    def scores(rows, mask_fn):
        for h, cols in enumerate(head_cols):
            s = lax.dot_general(k_ref[rows, cols], qs_ref[:, cols], _NT,
                                preferred_element_type=jnp.float32)
            s = mask_fn(h, s)
            s_ref[h] = s
            mblk_ref[pl.ds(h, 1), :] = jnp.max(s, axis=0, keepdims=True)

    def softmax_pv(rows, first):
        for h, cols in enumerate(head_cols):
            hrow = pl.ds(h, 1)
            if first:
                m_new = mblk_ref[hrow, :]
            else:
                m_old = m_ref[hrow, :]
                m_new = jnp.maximum(m_old, mblk_ref[hrow, :])
                alpha = jnp.exp2(m_old - m_new)
            p = jnp.exp2(s_ref[h] - m_new)
            psum = jnp.sum(p, axis=0, keepdims=True)
            pv = lax.dot_general(v_ref[rows, cols], p.astype(jnp.bfloat16), _TN,
                                 preferred_element_type=jnp.float32)
            m_ref[hrow, :] = m_new
            if first:
                l_ref[hrow, :] = psum
                acc_ref[h] = pv
            else:
                l_ref[hrow, :] = alpha * l_ref[hrow, :] + psum
                acc_ref[h] = alpha * acc_ref[h] + pv

    for h, cols in enumerate(head_cols):
        q = q_ref[:, cols]

        gate = lax.dot_general(kmean_ref[0, :, cols], q, _NT, precision=lax.Precision.HIGHEST,
                               preferred_element_type=jnp.float32)
        gate = jnp.where(past, gate, NEG_INF)
        picked = jnp.zeros(gate.shape, jnp.float32)
        for _ in range(min(MOBA_TOPK, nb)):
            best = jnp.max(gate, axis=0, keepdims=True)
            first = jnp.min(jnp.where(gate == best, blk, nb), axis=0, keepdims=True)
            hit = blk == first
            picked = jnp.where(hit, 1.0, picked)
            gate = jnp.where(hit, -jnp.inf, gate)
        bias_ref[h] = jnp.where(past, jnp.where(picked > 0.0, 0.0, NEG_INF), NEG_INF)

        qs_ref[:, cols] = (q * (HEAD_DIM ** -0.5 * LOG2_E)).astype(jnp.bfloat16)

    key_pos = lax.broadcasted_iota(jnp.int32, (MOBA_BLOCK, MOBA_BLOCK), 0)
    q_pos = lax.broadcasted_iota(jnp.int32, (MOBA_BLOCK, MOBA_BLOCK), 1)
    causal = key_pos <= q_pos
    scores(own, lambda h, s: jnp.where(causal, s, NEG_INF))

    @pl.when(qi >= 0)
    def _():
        softmax_pv(own, first=True)

    def past_block(j, carry):
        rows = pl.ds(pl.multiple_of(j * MOBA_BLOCK, MOBA_BLOCK), MOBA_BLOCK)
        scores(rows, lambda h, s: s + bias_ref[h, pl.ds(j, 1), :])

        @pl.when(j >= 0)
        def _():
            softmax_pv(rows, first=False)
        return carry

    lax.fori_loop(0, qi, past_block, 0)

    for h, cols in enumerate(head_cols):
        o_ref[:, cols] = (acc_ref[h] / l_ref[pl.ds(h, 1), :]).T.astype(o_ref.dtype)


def _attention(q, k, v, kmean, batch, seq):
    nb = seq // MOBA_BLOCK
    hg = ATTN_HEADS_PER_STEP
    width = hg * HEAD_DIM
    return pl.pallas_call(
        _attn_kernel,
        grid=(batch, N_ATT_HEADS // hg, nb),
        in_specs=[
            pl.BlockSpec((MOBA_BLOCK, width), lambda b, g, i: (b * nb + i, g)),
            pl.BlockSpec((seq, width), lambda b, g, i: (b, g)),
            pl.BlockSpec((seq, width), lambda b, g, i: (b, g)),
            pl.BlockSpec((1, nb, width), lambda b, g, i: (b, 0, g)),
        ],
        out_specs=pl.BlockSpec((MOBA_BLOCK, width), lambda b, g, i: (b * nb + i, g)),
        out_shape=jax.ShapeDtypeStruct((batch * seq, ATT_WIDTH), jnp.float32),
        scratch_shapes=[
            pltpu.VMEM((MOBA_BLOCK, width), jnp.bfloat16),
            pltpu.VMEM((hg, nb, MOBA_BLOCK), jnp.float32),
            pltpu.VMEM((hg, MOBA_BLOCK, MOBA_BLOCK), jnp.float32),
            pltpu.VMEM((hg, MOBA_BLOCK), jnp.float32),
            pltpu.VMEM((hg, MOBA_BLOCK), jnp.float32),
            pltpu.VMEM((hg, MOBA_BLOCK), jnp.float32),
            pltpu.VMEM((hg, HEAD_DIM, MOBA_BLOCK), jnp.float32),
        ],
        compiler_params=pltpu.CompilerParams(
            dimension_semantics=("arbitrary", "arbitrary", "arbitrary"),
            vmem_limit_bytes=VMEM_LIMIT),
        name="moba_attention",
    )(q, k, v, kmean)


def _tail_kernel(att_ref, sza_ref, gu_ref, vg_ref, szb_ref, sga_ref, sgb_ref, x_ref,
                 ws_ref, bst_ref, wpa_ref, wpb_ref, wout_ref, o_ref, b_in_ref):
    n_chunks = TAIL_ROWS // SGU_CHUNK
    a_in = (att_ref[...] * sza_ref[...].astype(jnp.float32)).astype(jnp.bfloat16)
    y_a = jnp.dot(a_in, wpa_ref[...], preferred_element_type=jnp.float32)

    t_pos = lax.broadcasted_iota(jnp.int32, (SGU_CHUNK, SGU_CHUNK), 0)
    s_pos = lax.broadcasted_iota(jnp.int32, (SGU_CHUNK, SGU_CHUNK), 1)
    causal = s_pos <= t_pos
    for g in range(N_SGU_GROUPS):
        cols = slice(g * SGU_GROUP_DIM, (g + 1) * SGU_GROUP_DIM)
        w = jnp.where(causal, ws_ref[g], 0.0).astype(jnp.bfloat16)
        vcat = jnp.concatenate(
            [vg_ref[c * SGU_CHUNK:(c + 1) * SGU_CHUNK, cols] for c in range(n_chunks)], axis=1)
        mixed = jnp.dot(w, vcat, preferred_element_type=jnp.float32) + bst_ref[:, g:g + 1]
        for c in range(n_chunks):
            rows = slice(c * SGU_CHUNK, (c + 1) * SGU_CHUNK)
            sgu = gu_ref[rows, cols].astype(jnp.float32) * mixed[:, c * SGU_CHUNK:(c + 1) * SGU_CHUNK]
            b_in_ref[rows, cols] = (sgu * szb_ref[rows, cols].astype(jnp.float32)).astype(jnp.bfloat16)
    y_b = jnp.dot(b_in_ref[...], wpb_ref[...], preferred_element_type=jnp.float32)

    merged = (sga_ref[...].astype(jnp.float32) * y_a
              + sgb_ref[...].astype(jnp.float32) * y_b).astype(jnp.bfloat16)
    o_ref[...] = x_ref[...] + jnp.dot(merged, wout_ref[...], preferred_element_type=jnp.float32)


def _tail(att, act, x2, w_s, b_s_t, wpa, wpb, wout):
    t = x2.shape[0]
    act_spec = lambda width, col: pl.BlockSpec((TAIL_ROWS, width), lambda i: (i, col))
    const = lambda shape: pl.BlockSpec(shape, lambda i: (0,) * len(shape),
                                       pipeline_mode=pl.Buffered(1))
    return pl.pallas_call(
        _tail_kernel,
        grid=(t // TAIL_ROWS,),
        in_specs=[
            act_spec(ATT_WIDTH, 0),
            act_spec(SEG, 0),
            act_spec(SEG, 1),
            act_spec(SEG, 2),
            act_spec(SEG, 3),
            act_spec(D_MODEL, 2),
            act_spec(D_MODEL, 3),
            act_spec(D_MODEL, 0),
            const((N_SGU_GROUPS, SGU_CHUNK, SGU_CHUNK)),
            const((SGU_CHUNK, N_SGU_GROUPS)),
            const((ATT_WIDTH, D_MODEL)),
            const((SGU_WIDTH, D_MODEL)),
            const((D_MODEL, D_MODEL)),
        ],
        out_specs=pl.BlockSpec((TAIL_ROWS, D_MODEL), lambda i: (i, 0)),
        out_shape=jax.ShapeDtypeStruct((t, D_MODEL), jnp.float32),
        scratch_shapes=[pltpu.VMEM((TAIL_ROWS, SGU_WIDTH), jnp.bfloat16)],
        compiler_params=pltpu.CompilerParams(
            dimension_semantics=("arbitrary",), vmem_limit_bytes=VMEM_LIMIT),
        name="tail",
    )(att, act, act, act, act, act, act, x2, w_s, b_s_t, wpa, wpb, wout)


def _layer(x2, batch, seq, norm_g, w_in, q_norm_g, k_norm_g, sgu_norm_g, w_s, b_s,
           w_proj_a, w_proj_b, w_out):
    bf16 = jnp.bfloat16
    h = _prenorm(x2, norm_g)
    q, k, kmean, v, act = _proj(h, w_in.astype(bf16), q_norm_g, k_norm_g, sgu_norm_g)
    kmean = kmean.reshape(batch, seq // MOBA_BLOCK, ATT_WIDTH)
    att = _attention(q, k, v, kmean, batch, seq)
    return _tail(att, act, x2, w_s, b_s.T, w_proj_a.astype(bf16), w_proj_b.astype(bf16),
                 w_out.astype(bf16))


def kernel(x, norm_g, w_in, q_norm_g, k_norm_g, sgu_norm_g, w_spatial, b_spatial,
           w_proj_a, w_proj_b, w_out):
    batch, seq, d = x.shape
    x2 = x.reshape(batch * seq, d)
    for l in range(norm_g.shape[0]):
        x2 = _layer(x2, batch, seq, norm_g[l], w_in[l], q_norm_g[l], k_norm_g[l],
                    sgu_norm_g[l], w_spatial[l], b_spatial[l], w_proj_a[l], w_proj_b[l],
                    w_out[l])
    return x2.reshape(batch, seq, d)
```

```python
import functools

import jax
import jax.numpy as jnp
from jax import lax
from jax.experimental import pallas as pl
from jax.experimental.pallas import tpu as pltpu

D_MODEL = 2048
N_ATT_HEADS = 8
HEAD_DIM = 128
ATT_WIDTH = N_ATT_HEADS * HEAD_DIM
MOBA_BLOCK = 256
MOBA_TOPK = 3
N_SGU_GROUPS = 8
SGU_GROUP_DIM = 128
SGU_WIDTH = N_SGU_GROUPS * SGU_GROUP_DIM
SGU_CHUNK = 128
IN_WIDTH = 4 * ATT_WIDTH + 3 * SGU_WIDTH + 2 * D_MODEL
EPS = 1e-6
NEG_INF = -1e30
LOG2_E = 1.4426950408889634

SEG = 1024
N_SEG = IN_WIDTH // SEG
ACT_WIDTH = IN_WIDTH - 3 * SEG

NORM_ROWS = 512
PROJ_ROWS = 1024
PROJ_CHUNK = 256
TAIL_ROWS = 256
ATTN_HEADS_PER_STEP = 8

VMEM_LIMIT = 56 * 1024 * 1024

_NT = (((1,), (1,)), ((), ()))
_TN = (((0,), (0,)), ((), ()))


def _prenorm_kernel(x_ref, g_ref, h_ref):
    x = x_ref[...]
    ms = jnp.mean(x * x, axis=-1, keepdims=True)
    h_ref[...] = (x * lax.rsqrt(ms + EPS) * g_ref[...]).astype(h_ref.dtype)


def _prenorm(x2, g):
    t = x2.shape[0]
    return pl.pallas_call(
        _prenorm_kernel,
        grid=(t // NORM_ROWS,),
        in_specs=[
            pl.BlockSpec((NORM_ROWS, D_MODEL), lambda i: (i, 0)),
            pl.BlockSpec((1, D_MODEL), lambda i: (0, 0)),
        ],
        out_specs=pl.BlockSpec((NORM_ROWS, D_MODEL), lambda i: (i, 0)),
        out_shape=jax.ShapeDtypeStruct((t, D_MODEL), jnp.bfloat16),
        compiler_params=pltpu.CompilerParams(
            dimension_semantics=("arbitrary",), vmem_limit_bytes=VMEM_LIMIT),
        name="prenorm",
    )(x2, g.reshape(1, D_MODEL))


def _group_rms(y, g_row):
    outs = []
    for c in range(SEG // HEAD_DIM):
        yc = y[:, c * HEAD_DIM:(c + 1) * HEAD_DIM]
        ms = jnp.mean(yc * yc, axis=-1, keepdims=True)
        outs.append(yc * lax.rsqrt(ms + EPS) * g_row)
    return jnp.concatenate(outs, axis=-1)


def _proj_kernel(h_ref, w_ref, qg_ref, kg_ref, sg_ref,
                 q_ref, k_ref, kmean_ref, v_ref, act_ref):
    j = pl.program_id(1)
    n_chunks = PROJ_ROWS // PROJ_CHUNK

    def chunk_dot(r):
        rows = pl.ds(pl.multiple_of(r * PROJ_CHUNK, PROJ_CHUNK), PROJ_CHUNK)
        return rows, jnp.dot(h_ref[rows, :], w_ref[...],
                             preferred_element_type=jnp.float32)

    def for_chunks(fn):
        def body(r, carry):
            rows, y = chunk_dot(r)
            fn(r, rows, y)
            return carry
        lax.fori_loop(0, n_chunks, body, 0)

    @pl.when(j == 0)
    def _():
        def fn(r, rows, y):
            q_ref[rows, :] = _group_rms(y, qg_ref[...])
        for_chunks(fn)

    @pl.when(j == 1)
    def _():
        def fn(r, rows, y):
            kn = _group_rms(y, kg_ref[...])
            k_ref[rows, :] = kn.astype(k_ref.dtype)
            kmean_ref[0, pl.ds(r, 1), :] = jnp.mean(kn, axis=0, keepdims=True)
        for_chunks(fn)

    @pl.when(j == 2)
    def _():
        def fn(r, rows, y):
            v_ref[rows, :] = y.astype(v_ref.dtype)
        for_chunks(fn)

    @pl.when((j == 3) | (j == 6))
    def _():
        def fn(r, rows, y):
            act_ref[rows, :] = jax.nn.silu(y).astype(act_ref.dtype)
        for_chunks(fn)

    @pl.when(j == 4)
    def _():
        def fn(r, rows, y):
            act_ref[rows, :] = jax.nn.gelu(y, approximate=True).astype(act_ref.dtype)
        for_chunks(fn)

    @pl.when(j == 5)
    def _():
        def fn(r, rows, y):
            vg = _group_rms(jax.nn.gelu(y, approximate=True), sg_ref[...])
            act_ref[rows, :] = vg.astype(act_ref.dtype)
        for_chunks(fn)

    @pl.when(j >= 7)
    def _():
        def fn(r, rows, y):
            act_ref[rows, :] = jax.nn.sigmoid(y).astype(act_ref.dtype)
        for_chunks(fn)


def _proj(h, w_bf, qg, kg, sg):
    t = h.shape[0]
    n_row_tiles = t // PROJ_ROWS
    blocks_per_tile = PROJ_ROWS // MOBA_BLOCK
    row_spec = lambda width: pl.BlockSpec((PROJ_ROWS, width), lambda i, j: (i, 0))
    gain_spec = pl.BlockSpec((1, HEAD_DIM), lambda i, j: (0, 0))
    return pl.pallas_call(
        _proj_kernel,
        grid=(n_row_tiles, N_SEG),
        in_specs=[
            row_spec(D_MODEL),
            pl.BlockSpec((D_MODEL, SEG), lambda i, j: (0, j)),
            gain_spec, gain_spec, gain_spec,
        ],
        out_specs=[
            row_spec(SEG),
            row_spec(SEG),
            pl.BlockSpec((1, blocks_per_tile, SEG), lambda i, j: (i, 0, 0)),
            row_spec(SEG),
            pl.BlockSpec((PROJ_ROWS, SEG), lambda i, j: (i, jnp.maximum(j - 3, 0))),
        ],
        out_shape=[
            jax.ShapeDtypeStruct((t, SEG), jnp.float32),
            jax.ShapeDtypeStruct((t, SEG), jnp.bfloat16),
            jax.ShapeDtypeStruct((n_row_tiles, blocks_per_tile, SEG), jnp.float32),
            jax.ShapeDtypeStruct((t, SEG), jnp.bfloat16),
            jax.ShapeDtypeStruct((t, ACT_WIDTH), jnp.bfloat16),
        ],
        compiler_params=pltpu.CompilerParams(
            dimension_semantics=("arbitrary", "arbitrary"),
            vmem_limit_bytes=VMEM_LIMIT),
        name="in_proj",
    )(h, w_bf, qg.reshape(1, HEAD_DIM), kg.reshape(1, HEAD_DIM), sg.reshape(1, SGU_GROUP_DIM))


def _split_bf16(a):
    hi = a.astype(jnp.bfloat16)
    lo = (a - hi.astype(jnp.float32)).astype(jnp.bfloat16)
    return hi, lo


def _attn_kernel(q_ref, k_ref, v_ref, kmean_ref, o_ref,
                 qa_ref, s_ref, mblk_ref, m_ref, l_ref, acc_ref):
    qi = pl.program_id(2)
    nb = kmean_ref.shape[1]
    n_heads = q_ref.shape[1] // HEAD_DIM
    head_cols = [slice(h * HEAD_DIM, (h + 1) * HEAD_DIM) for h in range(n_heads)]
    blk = lax.broadcasted_iota(jnp.int32, (nb, MOBA_BLOCK), 0)
    past = blk < qi

    def block_rows(b):
        return pl.ds(pl.multiple_of(b * MOBA_BLOCK, MOBA_BLOCK), MOBA_BLOCK)

    def visit_rows(t):
        return block_rows(jnp.where(t == 0, qi, t - 1))

    def put_scores(slot, h, s):
        s_ref[slot, h] = s
        mblk_ref[slot, pl.ds(h, 1), :] = jnp.max(s, axis=0, keepdims=True)

    def own_scores(slot):
        key_pos = lax.broadcasted_iota(jnp.int32, (MOBA_BLOCK, MOBA_BLOCK), 0)
        q_pos = lax.broadcasted_iota(jnp.int32, (MOBA_BLOCK, MOBA_BLOCK), 1)
        causal = key_pos <= q_pos
        rows = block_rows(qi)
        for h, cols in enumerate(head_cols):
            s = lax.dot_general(k_ref[rows, cols], qa_ref[:, 2 * h * HEAD_DIM:(2 * h + 1) * HEAD_DIM],
                                _NT, preferred_element_type=jnp.float32)
            put_scores(slot, h, jnp.where(causal, s, NEG_INF))

    def past_scores(slot, j):
        rows = block_rows(j)
        lane = lax.broadcasted_iota(jnp.int32, (MOBA_BLOCK, HEAD_DIM), 1)
        block_lane = jnp.where(lane == j, 1.0, 0.0).astype(jnp.bfloat16)
        for h, cols in enumerate(head_cols):
            k_aug = jnp.concatenate([k_ref[rows, cols], block_lane], axis=1)
            s = lax.dot_general(k_aug, qa_ref[:, 2 * h * HEAD_DIM:(2 * h + 2) * HEAD_DIM],
                                _NT, preferred_element_type=jnp.float32)
            put_scores(slot, h, s)

    def softmax_pv(slot, rows):
        for h, cols in enumerate(head_cols):
            hrow = pl.ds(h, 1)
            m_old = m_ref[hrow, :]
            m_new = jnp.maximum(m_old, mblk_ref[slot, hrow, :])
            alpha = jnp.exp2(m_old - m_new)
            p = jnp.exp2(s_ref[slot, h] - m_new)
            psum = jnp.sum(p, axis=0, keepdims=True)
            pv = lax.dot_general(v_ref[rows, cols], p.astype(jnp.bfloat16), _TN,
                                 preferred_element_type=jnp.float32)
            m_ref[hrow, :] = m_new
            l_ref[hrow, :] = alpha * l_ref[hrow, :] + psum
            acc_ref[h] = alpha * acc_ref[h] + pv

    for h, cols in enumerate(head_cols):
        q = q_ref[:, cols]

        km = kmean_ref[0, :, cols]
        q_hi, q_lo = _split_bf16(q)
        km_hi, km_lo = _split_bf16(km)
        g_hi = lax.dot_general(jnp.concatenate([km_hi, km_lo], axis=0), q_hi, _NT,
                               preferred_element_type=jnp.float32)
        g_lo = lax.dot_general(km_hi, q_lo, _NT, preferred_element_type=jnp.float32)
        gate = g_hi[:nb] + g_hi[nb:] + g_lo
        gate = jnp.where(past, gate, NEG_INF)
        picked = jnp.zeros(gate.shape, jnp.float32)
        for _ in range(min(MOBA_TOPK, nb)):
            best = jnp.max(gate, axis=0, keepdims=True)
            first = jnp.min(jnp.where(gate == best, blk, nb), axis=0, keepdims=True)
            hit = blk == first
            picked = jnp.where(hit, 1.0, picked)
            gate = jnp.where(hit, -jnp.inf, gate)
        mask = jnp.where(past, jnp.where(picked > 0.0, 0.0, NEG_INF), NEG_INF)
        mask_q = jnp.concatenate(
            [mask, jnp.zeros((HEAD_DIM - nb, MOBA_BLOCK), jnp.float32)], axis=0).T

        qa_ref[:, 2 * h * HEAD_DIM:(2 * h + 1) * HEAD_DIM] = (
            q * (HEAD_DIM ** -0.5 * LOG2_E)).astype(jnp.bfloat16)
        qa_ref[:, (2 * h + 1) * HEAD_DIM:(2 * h + 2) * HEAD_DIM] = mask_q.astype(jnp.bfloat16)

    m_ref[...] = jnp.full(m_ref.shape, -jnp.inf, jnp.float32)
    l_ref[...] = jnp.zeros(l_ref.shape, jnp.float32)
    acc_ref[...] = jnp.zeros(acc_ref.shape, jnp.float32)

    own_scores(0)

    def visit_pair(u, carry):
        t = 2 * u
        past_scores(1, t)
        softmax_pv(0, visit_rows(t))
        past_scores(0, t + 1)
        softmax_pv(1, visit_rows(t + 1))
        return carry

    lax.fori_loop(0, lax.shift_right_logical(qi, 1), visit_pair, 0)
    odd = (qi & 1) == 1

    @pl.when(odd)
    def _():
        past_scores(1, qi - 1)
        softmax_pv(0, visit_rows(qi - 1))

    @pl.when(odd)
    def _():
        softmax_pv(1, visit_rows(qi))

    @pl.when(jnp.logical_not(odd))
    def _():
        softmax_pv(0, visit_rows(qi))

    for h, cols in enumerate(head_cols):
        o_ref[:, cols] = (acc_ref[h] / l_ref[pl.ds(h, 1), :]).T.astype(o_ref.dtype)


def _attention(q, k, v, kmean, batch, seq):
    nb = seq // MOBA_BLOCK
    hg = ATTN_HEADS_PER_STEP
    width = hg * HEAD_DIM
    return pl.pallas_call(
        _attn_kernel,
        grid=(batch, N_ATT_HEADS // hg, nb),
        in_specs=[
            pl.BlockSpec((MOBA_BLOCK, width), lambda b, g, i: (b * nb + i, g)),
            pl.BlockSpec((seq, width), lambda b, g, i: (b, g)),
            pl.BlockSpec((seq, width), lambda b, g, i: (b, g)),
            pl.BlockSpec((1, nb, width), lambda b, g, i: (b, 0, g)),
        ],
        out_specs=pl.BlockSpec((MOBA_BLOCK, width), lambda b, g, i: (b * nb + i, g)),
        out_shape=jax.ShapeDtypeStruct((batch * seq, ATT_WIDTH), jnp.float32),
        scratch_shapes=[
            pltpu.VMEM((MOBA_BLOCK, 2 * width), jnp.bfloat16),
            pltpu.VMEM((2, hg, MOBA_BLOCK, MOBA_BLOCK), jnp.float32),
            pltpu.VMEM((2, hg, MOBA_BLOCK), jnp.float32),
            pltpu.VMEM((hg, MOBA_BLOCK), jnp.float32),
            pltpu.VMEM((hg, MOBA_BLOCK), jnp.float32),
            pltpu.VMEM((hg, HEAD_DIM, MOBA_BLOCK), jnp.float32),
        ],
        compiler_params=pltpu.CompilerParams(
            dimension_semantics=("arbitrary", "arbitrary", "arbitrary"),
            vmem_limit_bytes=VMEM_LIMIT),
        name="moba_attention",
    )(q, k, v, kmean)


def _tail_kernel(att_ref, sza_ref, gu_ref, vg_ref, szb_ref, sga_ref, sgb_ref, x_ref,
                 ws_ref, bst_ref, wpa_ref, wpb_ref, wout_ref, o_ref, b_in_ref):
    n_chunks = TAIL_ROWS // SGU_CHUNK
    a_in = (att_ref[...] * sza_ref[...].astype(jnp.float32)).astype(jnp.bfloat16)
    y_a = jnp.dot(a_in, wpa_ref[...], preferred_element_type=jnp.float32)

    t_pos = lax.broadcasted_iota(jnp.int32, (SGU_CHUNK, SGU_CHUNK), 0)
    s_pos = lax.broadcasted_iota(jnp.int32, (SGU_CHUNK, SGU_CHUNK), 1)
    causal = s_pos <= t_pos
    for g in range(N_SGU_GROUPS):
        cols = slice(g * SGU_GROUP_DIM, (g + 1) * SGU_GROUP_DIM)
        w = jnp.where(causal, ws_ref[g], 0.0).astype(jnp.bfloat16)
        vcat = jnp.concatenate(
            [vg_ref[c * SGU_CHUNK:(c + 1) * SGU_CHUNK, cols] for c in range(n_chunks)], axis=1)
        mixed = jnp.dot(w, vcat, preferred_element_type=jnp.float32) + bst_ref[:, g:g + 1]
        for c in range(n_chunks):
            rows = slice(c * SGU_CHUNK, (c + 1) * SGU_CHUNK)
            sgu = gu_ref[rows, cols].astype(jnp.float32) * mixed[:, c * SGU_CHUNK:(c + 1) * SGU_CHUNK]
            b_in_ref[rows, cols] = (sgu * szb_ref[rows, cols].astype(jnp.float32)).astype(jnp.bfloat16)
    y_b = jnp.dot(b_in_ref[...], wpb_ref[...], preferred_element_type=jnp.float32)

    merged = (sga_ref[...].astype(jnp.float32) * y_a
              + sgb_ref[...].astype(jnp.float32) * y_b).astype(jnp.bfloat16)
    o_ref[...] = x_ref[...] + jnp.dot(merged, wout_ref[...], preferred_element_type=jnp.float32)


def _tail(att, act, x2, w_s, b_s_t, wpa, wpb, wout):
    t = x2.shape[0]
    act_spec = lambda width, col: pl.BlockSpec((TAIL_ROWS, width), lambda i: (i, col))
    const = lambda shape: pl.BlockSpec(shape, lambda i: (0,) * len(shape),
                                       pipeline_mode=pl.Buffered(1))
    return pl.pallas_call(
        _tail_kernel,
        grid=(t // TAIL_ROWS,),
        in_specs=[
            act_spec(ATT_WIDTH, 0),
            act_spec(SEG, 0),
            act_spec(SEG, 1),
            act_spec(SEG, 2),
            act_spec(SEG, 3),
            act_spec(D_MODEL, 2),
            act_spec(D_MODEL, 3),
            act_spec(D_MODEL, 0),
            const((N_SGU_GROUPS, SGU_CHUNK, SGU_CHUNK)),
            const((SGU_CHUNK, N_SGU_GROUPS)),
            const((ATT_WIDTH, D_MODEL)),
            const((SGU_WIDTH, D_MODEL)),
            const((D_MODEL, D_MODEL)),
        ],
        out_specs=pl.BlockSpec((TAIL_ROWS, D_MODEL), lambda i: (i, 0)),
        out_shape=jax.ShapeDtypeStruct((t, D_MODEL), jnp.float32),
        scratch_shapes=[pltpu.VMEM((TAIL_ROWS, SGU_WIDTH), jnp.bfloat16)],
        compiler_params=pltpu.CompilerParams(
            dimension_semantics=("arbitrary",), vmem_limit_bytes=VMEM_LIMIT),
        name="tail",
    )(att, act, act, act, act, act, act, x2, w_s, b_s_t, wpa, wpb, wout)


def _layer(x2, batch, seq, norm_g, w_in, q_norm_g, k_norm_g, sgu_norm_g, w_s, b_s,
           w_proj_a, w_proj_b, w_out):
    bf16 = jnp.bfloat16
    h = _prenorm(x2, norm_g)
    q, k, kmean, v, act = _proj(h, w_in.astype(bf16), q_norm_g, k_norm_g, sgu_norm_g)
    kmean = kmean.reshape(batch, seq // MOBA_BLOCK, ATT_WIDTH)
    att = _attention(q, k, v, kmean, batch, seq)
    return _tail(att, act, x2, w_s, b_s.T, w_proj_a.astype(bf16), w_proj_b.astype(bf16),
                 w_out.astype(bf16))


def kernel(x, norm_g, w_in, q_norm_g, k_norm_g, sgu_norm_g, w_spatial, b_spatial,
           w_proj_a, w_proj_b, w_out):
    batch, seq, d = x.shape
    x2 = x.reshape(batch * seq, d)
    for l in range(norm_g.shape[0]):
        x2 = _layer(x2, batch, seq, norm_g[l], w_in[l], q_norm_g[l], k_norm_g[l],
                    sgu_norm_g[l], w_spatial[l], b_spatial[l], w_proj_a[l], w_proj_b[l],
                    w_out[l])
    return x2.reshape(batch, seq, d)
```

```python
import functools

import jax
import jax.numpy as jnp
from jax import lax
from jax.experimental import pallas as pl
from jax.experimental.pallas import tpu as pltpu

D_MODEL = 2048
N_ATT_HEADS = 8
HEAD_DIM = 128
ATT_WIDTH = N_ATT_HEADS * HEAD_DIM
MOBA_BLOCK = 256
MOBA_TOPK = 3
N_SGU_GROUPS = 8
SGU_GROUP_DIM = 128
SGU_WIDTH = N_SGU_GROUPS * SGU_GROUP_DIM
SGU_CHUNK = 128
IN_WIDTH = 4 * ATT_WIDTH + 3 * SGU_WIDTH + 2 * D_MODEL
EPS = 1e-6
NEG_INF = -1e30
LOG2_E = 1.4426950408889634

SEG = 1024
N_SEG = IN_WIDTH // SEG
ACT_WIDTH = IN_WIDTH - 3 * SEG

PROJ_ROWS = 1024
PROJ_CHUNK = 256
TAIL_ROWS = 256
ATTN_HEADS_PER_STEP = 8

VMEM_LIMIT = 56 * 1024 * 1024

_NT = (((1,), (1,)), ((), ()))
_TN = (((0,), (0,)), ((), ()))


def _group_rms(y, g_row):
    outs = []
    for c in range(SEG // HEAD_DIM):
        yc = y[:, c * HEAD_DIM:(c + 1) * HEAD_DIM]
        ms = jnp.mean(yc * yc, axis=-1, keepdims=True)
        outs.append(yc * lax.rsqrt(ms + EPS) * g_row)
    return jnp.concatenate(outs, axis=-1)


def _proj_kernel(x_ref, ng_ref, w_ref, qg_ref, kg_ref, sg_ref,
                 q_ref, k_ref, kmean_ref, v_ref, act_ref, h_ref):
    j = pl.program_id(1)
    n_chunks = PROJ_ROWS // PROJ_CHUNK

    def prenorm(rows):
        x = x_ref[rows, :]
        ms = jnp.mean(x * x, axis=-1, keepdims=True)
        h = (x * lax.rsqrt(ms + EPS) * ng_ref[...]).astype(h_ref.dtype)
        h_ref[rows, :] = h
        return h

    def for_chunks(fn, lhs=lambda rows: h_ref[rows, :]):
        def body(r, carry):
            rows = pl.ds(pl.multiple_of(r * PROJ_CHUNK, PROJ_CHUNK), PROJ_CHUNK)
            fn(r, rows, jnp.dot(lhs(rows), w_ref[...], preferred_element_type=jnp.float32))
            return carry
        lax.fori_loop(0, n_chunks, body, 0, unroll=True)

    @pl.when(j == 0)
    def _():
        def fn(r, rows, y):
            q_ref[rows, :] = _group_rms(y, qg_ref[...])
        for_chunks(fn, lhs=prenorm)

    @pl.when(j == 1)
    def _():
        def fn(r, rows, y):
            kn = _group_rms(y, kg_ref[...])
            k_ref[rows, :] = kn.astype(k_ref.dtype)
            kmean_ref[0, pl.ds(r, 1), :] = jnp.mean(kn, axis=0, keepdims=True)
        for_chunks(fn)

    @pl.when(j == 2)
    def _():
        def fn(r, rows, y):
            v_ref[rows, :] = y.astype(v_ref.dtype)
        for_chunks(fn)

    @pl.when((j == 3) | (j == 6))
    def _():
        def fn(r, rows, y):
            act_ref[rows, :] = jax.nn.silu(y).astype(act_ref.dtype)
        for_chunks(fn)

    @pl.when(j == 4)
    def _():
        def fn(r, rows, y):
            act_ref[rows, :] = jax.nn.gelu(y, approximate=True).astype(act_ref.dtype)
        for_chunks(fn)

    @pl.when(j == 5)
    def _():
        def fn(r, rows, y):
            vg = _group_rms(jax.nn.gelu(y, approximate=True), sg_ref[...])
            act_ref[rows, :] = vg.astype(act_ref.dtype)
        for_chunks(fn)

    @pl.when(j >= 7)
    def _():
        def fn(r, rows, y):
            act_ref[rows, :] = jax.nn.sigmoid(y).astype(act_ref.dtype)
        for_chunks(fn)


def _proj(x2, ng, w_bf, qg, kg, sg):
    t = x2.shape[0]
    n_row_tiles = t // PROJ_ROWS
    blocks_per_tile = PROJ_ROWS // MOBA_BLOCK
    row_spec = lambda width: pl.BlockSpec((PROJ_ROWS, width), lambda i, j: (i, 0))
    gain_spec = pl.BlockSpec((1, HEAD_DIM), lambda i, j: (0, 0))
    return pl.pallas_call(
        _proj_kernel,
        grid=(n_row_tiles, N_SEG),
        in_specs=[
            row_spec(D_MODEL),
            pl.BlockSpec((1, D_MODEL), lambda i, j: (0, 0)),
            pl.BlockSpec((D_MODEL, SEG), lambda i, j: (0, j)),
            gain_spec, gain_spec, gain_spec,
        ],
        out_specs=[
            row_spec(SEG),
            row_spec(SEG),
            pl.BlockSpec((1, blocks_per_tile, SEG), lambda i, j: (i, 0, 0)),
            row_spec(SEG),
            pl.BlockSpec((PROJ_ROWS, SEG), lambda i, j: (i, jnp.maximum(j - 3, 0))),
        ],
        out_shape=[
            jax.ShapeDtypeStruct((t, SEG), jnp.float32),
            jax.ShapeDtypeStruct((t, SEG), jnp.bfloat16),
            jax.ShapeDtypeStruct((n_row_tiles, blocks_per_tile, SEG), jnp.float32),
            jax.ShapeDtypeStruct((t, SEG), jnp.bfloat16),
            jax.ShapeDtypeStruct((t, ACT_WIDTH), jnp.bfloat16),
        ],
        scratch_shapes=[pltpu.VMEM((PROJ_ROWS, D_MODEL), jnp.bfloat16)],
        compiler_params=pltpu.CompilerParams(
            dimension_semantics=("arbitrary", "arbitrary"),
            vmem_limit_bytes=VMEM_LIMIT),
        name="in_proj",
    )(x2, ng.reshape(1, D_MODEL), w_bf, qg.reshape(1, HEAD_DIM), kg.reshape(1, HEAD_DIM),
      sg.reshape(1, SGU_GROUP_DIM))


def _split_bf16(a):
    hi = a.astype(jnp.bfloat16)
    lo = (a - hi.astype(jnp.float32)).astype(jnp.bfloat16)
    return hi, lo


def _attn_kernel(q_ref, k_ref, v_ref, kmean_ref, o_ref,
                 qa_ref, s_ref, mblk_ref, m_ref, l_ref, acc_ref):
    qi = pl.program_id(2)
    nb = kmean_ref.shape[1]
    n_heads = q_ref.shape[1] // HEAD_DIM
    head_cols = [slice(h * HEAD_DIM, (h + 1) * HEAD_DIM) for h in range(n_heads)]
    blk = lax.broadcasted_iota(jnp.int32, (nb, MOBA_BLOCK), 0)
    past = blk < qi

    def block_rows(b):
        return pl.ds(pl.multiple_of(b * MOBA_BLOCK, MOBA_BLOCK), MOBA_BLOCK)

    def visit_rows(t):
        return block_rows(jnp.where(t == 0, qi, t - 1))

    def put_scores(slot, h, s):
        s_ref[slot, h] = s
        mblk_ref[slot, pl.ds(h, 1), :] = jnp.max(s, axis=0, keepdims=True)

    def own_scores(slot):
        key_pos = lax.broadcasted_iota(jnp.int32, (MOBA_BLOCK, MOBA_BLOCK), 0)
        q_pos = lax.broadcasted_iota(jnp.int32, (MOBA_BLOCK, MOBA_BLOCK), 1)
        causal = key_pos <= q_pos
        rows = block_rows(qi)
        for h, cols in enumerate(head_cols):
            s = lax.dot_general(k_ref[rows, cols], qa_ref[:, 2 * h * HEAD_DIM:(2 * h + 1) * HEAD_DIM],
                                _NT, preferred_element_type=jnp.float32)
            put_scores(slot, h, jnp.where(causal, s, NEG_INF))

    def past_scores(slot, j):
        rows = block_rows(j)
        lane = lax.broadcasted_iota(jnp.int32, (MOBA_BLOCK, HEAD_DIM), 1)
        block_lane = jnp.where(lane == j, 1.0, 0.0).astype(jnp.bfloat16)
        for h, cols in enumerate(head_cols):
            k_aug = jnp.concatenate([k_ref[rows, cols], block_lane], axis=1)
            s = lax.dot_general(k_aug, qa_ref[:, 2 * h * HEAD_DIM:(2 * h + 2) * HEAD_DIM],
                                _NT, preferred_element_type=jnp.float32)
            put_scores(slot, h, s)

    def softmax_pv(slot, rows):
        for h, cols in enumerate(head_cols):
            hrow = pl.ds(h, 1)
            m_old = m_ref[hrow, :]
            m_new = jnp.maximum(m_old, mblk_ref[slot, hrow, :])
            alpha = jnp.exp2(m_old - m_new)
            p = jnp.exp2(s_ref[slot, h] - m_new)
            psum = jnp.sum(p, axis=0, keepdims=True)
            pv = lax.dot_general(v_ref[rows, cols], p.astype(jnp.bfloat16), _TN,
                                 preferred_element_type=jnp.float32)
            m_ref[hrow, :] = m_new
            l_ref[hrow, :] = alpha * l_ref[hrow, :] + psum
            acc_ref[h] = alpha * acc_ref[h] + pv

    for h, cols in enumerate(head_cols):
        q = q_ref[:, cols]

        km = kmean_ref[0, :, cols]
        q_hi, q_lo = _split_bf16(q)
        km_hi, km_lo = _split_bf16(km)
        g_hi = lax.dot_general(jnp.concatenate([km_hi, km_lo], axis=0), q_hi, _NT,
                               preferred_element_type=jnp.float32)
        g_lo = lax.dot_general(km_hi, q_lo, _NT, preferred_element_type=jnp.float32)
        gate = g_hi[:nb] + g_hi[nb:] + g_lo
        gate = jnp.where(past, gate, NEG_INF)
        picked = jnp.zeros(gate.shape, jnp.float32)
        for _ in range(min(MOBA_TOPK, nb)):
            best = jnp.max(gate, axis=0, keepdims=True)
            first = jnp.min(jnp.where(gate == best, blk, nb), axis=0, keepdims=True)
            hit = blk == first
            picked = jnp.where(hit, 1.0, picked)
            gate = jnp.where(hit, -jnp.inf, gate)
        mask = jnp.where(past, jnp.where(picked > 0.0, 0.0, NEG_INF), NEG_INF)
        mask_q = jnp.concatenate(
            [mask, jnp.zeros((HEAD_DIM - nb, MOBA_BLOCK), jnp.float32)], axis=0).T

        qa_ref[:, 2 * h * HEAD_DIM:(2 * h + 1) * HEAD_DIM] = (
            q * (HEAD_DIM ** -0.5 * LOG2_E)).astype(jnp.bfloat16)
        qa_ref[:, (2 * h + 1) * HEAD_DIM:(2 * h + 2) * HEAD_DIM] = mask_q.astype(jnp.bfloat16)

    m_ref[...] = jnp.full(m_ref.shape, -jnp.inf, jnp.float32)
    l_ref[...] = jnp.zeros(l_ref.shape, jnp.float32)
    acc_ref[...] = jnp.zeros(acc_ref.shape, jnp.float32)

    own_scores(0)

    def visit_pair(u, carry):
        t = 2 * u
        past_scores(1, t)
        softmax_pv(0, visit_rows(t))
        past_scores(0, t + 1)
        softmax_pv(1, visit_rows(t + 1))
        return carry

    lax.fori_loop(0, lax.shift_right_logical(qi, 1), visit_pair, 0)
    odd = (qi & 1) == 1

    @pl.when(odd)
    def _():
        past_scores(1, qi - 1)
        softmax_pv(0, visit_rows(qi - 1))

    @pl.when(odd)
    def _():
        softmax_pv(1, visit_rows(qi))

    @pl.when(jnp.logical_not(odd))
    def _():
        softmax_pv(0, visit_rows(qi))

    for h, cols in enumerate(head_cols):
        o_ref[:, cols] = (acc_ref[h] / l_ref[pl.ds(h, 1), :]).T.astype(o_ref.dtype)


def _attention(q, k, v, kmean, batch, seq):
    nb = seq // MOBA_BLOCK
    hg = ATTN_HEADS_PER_STEP
    width = hg * HEAD_DIM
    return pl.pallas_call(
        _attn_kernel,
        grid=(batch, N_ATT_HEADS // hg, nb),
        in_specs=[
            pl.BlockSpec((MOBA_BLOCK, width), lambda b, g, i: (b * nb + i, g)),
            pl.BlockSpec((seq, width), lambda b, g, i: (b, g)),
            pl.BlockSpec((seq, width), lambda b, g, i: (b, g)),
            pl.BlockSpec((1, nb, width), lambda b, g, i: (b, 0, g)),
        ],
        out_specs=pl.BlockSpec((MOBA_BLOCK, width), lambda b, g, i: (b * nb + i, g)),
        out_shape=jax.ShapeDtypeStruct((batch * seq, ATT_WIDTH), jnp.float32),
        scratch_shapes=[
            pltpu.VMEM((MOBA_BLOCK, 2 * width), jnp.bfloat16),
            pltpu.VMEM((2, hg, MOBA_BLOCK, MOBA_BLOCK), jnp.float32),
            pltpu.VMEM((2, hg, MOBA_BLOCK), jnp.float32),
            pltpu.VMEM((hg, MOBA_BLOCK), jnp.float32),
            pltpu.VMEM((hg, MOBA_BLOCK), jnp.float32),
            pltpu.VMEM((hg, HEAD_DIM, MOBA_BLOCK), jnp.float32),
        ],
        compiler_params=pltpu.CompilerParams(
            dimension_semantics=("arbitrary", "arbitrary", "arbitrary"),
            vmem_limit_bytes=VMEM_LIMIT),
        name="moba_attention",
    )(q, k, v, kmean)


def _tail_kernel(att_ref, sza_ref, gu_ref, vg_ref, szb_ref, sga_ref, sgb_ref, x_ref,
                 ws_ref, bst_ref, wpa_ref, wpb_ref, wout_ref, o_ref, b_in_ref):
    n_chunks = TAIL_ROWS // SGU_CHUNK
    a_in = (att_ref[...] * sza_ref[...].astype(jnp.float32)).astype(jnp.bfloat16)
    y_a = jnp.dot(a_in, wpa_ref[...], preferred_element_type=jnp.float32)

    t_pos = lax.broadcasted_iota(jnp.int32, (SGU_CHUNK, SGU_CHUNK), 0)
    s_pos = lax.broadcasted_iota(jnp.int32, (SGU_CHUNK, SGU_CHUNK), 1)
    causal = s_pos <= t_pos
    for g in range(N_SGU_GROUPS):
        cols = slice(g * SGU_GROUP_DIM, (g + 1) * SGU_GROUP_DIM)
        w = jnp.where(causal, ws_ref[g], 0.0).astype(jnp.bfloat16)
        vcat = jnp.concatenate(
            [vg_ref[c * SGU_CHUNK:(c + 1) * SGU_CHUNK, cols] for c in range(n_chunks)], axis=1)
        mixed = jnp.dot(w, vcat, preferred_element_type=jnp.float32) + bst_ref[:, g:g + 1]
        for c in range(n_chunks):
            rows = slice(c * SGU_CHUNK, (c + 1) * SGU_CHUNK)
            sgu = gu_ref[rows, cols].astype(jnp.float32) * mixed[:, c * SGU_CHUNK:(c + 1) * SGU_CHUNK]
            b_in_ref[rows, cols] = (sgu * szb_ref[rows, cols].astype(jnp.float32)).astype(jnp.bfloat16)
    y_b = jnp.dot(b_in_ref[...], wpb_ref[...], preferred_element_type=jnp.float32)

    merged = (sga_ref[...].astype(jnp.float32) * y_a
              + sgb_ref[...].astype(jnp.float32) * y_b).astype(jnp.bfloat16)
    o_ref[...] = x_ref[...] + jnp.dot(merged, wout_ref[...], preferred_element_type=jnp.float32)


def _tail(att, act, x2, w_s, b_s_t, wpa, wpb, wout):
    t = x2.shape[0]
    act_spec = lambda width, col: pl.BlockSpec((TAIL_ROWS, width), lambda i: (i, col))
    const = lambda shape: pl.BlockSpec(shape, lambda i: (0,) * len(shape),
                                       pipeline_mode=pl.Buffered(1))
    return pl.pallas_call(
        _tail_kernel,
        grid=(t // TAIL_ROWS,),
        in_specs=[
            act_spec(ATT_WIDTH, 0),
            act_spec(SEG, 0),
            act_spec(SEG, 1),
            act_spec(SEG, 2),
            act_spec(SEG, 3),
            act_spec(D_MODEL, 2),
            act_spec(D_MODEL, 3),
            act_spec(D_MODEL, 0),
            const((N_SGU_GROUPS, SGU_CHUNK, SGU_CHUNK)),
            const((SGU_CHUNK, N_SGU_GROUPS)),
            const((ATT_WIDTH, D_MODEL)),
            const((SGU_WIDTH, D_MODEL)),
            const((D_MODEL, D_MODEL)),
        ],
        out_specs=pl.BlockSpec((TAIL_ROWS, D_MODEL), lambda i: (i, 0)),
        out_shape=jax.ShapeDtypeStruct((t, D_MODEL), jnp.float32),
        scratch_shapes=[pltpu.VMEM((TAIL_ROWS, SGU_WIDTH), jnp.bfloat16)],
        compiler_params=pltpu.CompilerParams(
            dimension_semantics=("arbitrary",), vmem_limit_bytes=VMEM_LIMIT),
        name="tail",
    )(att, act, act, act, act, act, act, x2, w_s, b_s_t, wpa, wpb, wout)


def _layer(x2, batch, seq, norm_g, w_in, q_norm_g, k_norm_g, sgu_norm_g, w_s, b_s,
           w_proj_a, w_proj_b, w_out):
    bf16 = jnp.bfloat16
    q, k, kmean, v, act = _proj(x2, norm_g, w_in.astype(bf16), q_norm_g, k_norm_g, sgu_norm_g)
    kmean = kmean.reshape(batch, seq // MOBA_BLOCK, ATT_WIDTH)
    att = _attention(q, k, v, kmean, batch, seq)
    return _tail(att, act, x2, w_s, b_s.T, w_proj_a.astype(bf16), w_proj_b.astype(bf16),
                 w_out.astype(bf16))


def kernel(x, norm_g, w_in, q_norm_g, k_norm_g, sgu_norm_g, w_spatial, b_spatial,
           w_proj_a, w_proj_b, w_out):
    batch, seq, d = x.shape
    x2 = x.reshape(batch * seq, d)
    for l in range(norm_g.shape[0]):
        x2 = _layer(x2, batch, seq, norm_g[l], w_in[l], q_norm_g[l], k_norm_g[l],
                    sgu_norm_g[l], w_spatial[l], b_spatial[l], w_proj_a[l], w_proj_b[l],
                    w_out[l])
    return x2.reshape(batch, seq, d)
```
